```python
import math
import jax, jax.numpy as jnp
from jax import lax
import numpy as np

D_MODEL = 1024
BATCH = 16
SEQ = 4096
DEPTH = 2
DEC_BATCH = 2
DEC_SEQ = 8192
PAST_LEN = 128

N_MIXERS = 2
SSM_EXPAND = 2
D_INNER = SSM_EXPAND * D_MODEL
SSM_HEAD_DIM = 64
SSM_HEADS = D_INNER // SSM_HEAD_DIM
SSM_GROUPS = 8
SSM_STATE = 128
SSM_CONV = 3
SSM_CHUNK = 128
SSM_CONV_DIM = D_INNER + 2 * SSM_GROUPS * SSM_STATE
SSM_IN_DIM = D_INNER + SSM_CONV_DIM + 2 * SSM_HEADS
CF_DIM = D_MODEL
CF_CONV = 31
FFN_DIM = 2816
FFN_CONV = 3
EPS = 1e-6

kernel_name = "hybrid_bidir_ssd_conformer_convffn"


def _rmsnorm(x, w):
    xf = x.astype(jnp.float32)
    y = xf * lax.rsqrt(jnp.mean(xf * xf, axis=-1, keepdims=True) + EPS)
    return (y * w.astype(jnp.float32)).astype(x.dtype)


def _layernorm(x, w, b):
    xf = x.astype(jnp.float32)
    mu = jnp.mean(xf, axis=-1, keepdims=True)
    xc = xf - mu
    y = xc * lax.rsqrt(jnp.mean(xc * xc, axis=-1, keepdims=True) + EPS)
    return (y * w.astype(jnp.float32) + b.astype(jnp.float32)).astype(x.dtype)


def _dwconv(x, w, b):
    y = lax.conv_general_dilated(
        x, w[:, None, :].astype(x.dtype), window_strides=(1,), padding='SAME',
        dimension_numbers=('NWC', 'WIO', 'NWC'), feature_group_count=x.shape[-1])
    return y + b.astype(x.dtype)


def _ssd_scan(x, dt, A, B, C):
    b, L, H, P = x.shape
    G, N = B.shape[-2:]
    R = H // G
    Q = SSM_CHUNK
    nc = L // Q
    xdt = (x * dt[..., None]).reshape(b, nc, Q, G, R, P)
    dA = (dt * A).reshape(b, nc, Q, G, R)
    Bc = B.reshape(b, nc, Q, G, N)
    Cc = C.reshape(b, nc, Q, G, N)
    a_cs = jnp.cumsum(dA, axis=2)
    seg = a_cs[:, :, :, None] - a_cs[:, :, None, :]
    mask = jnp.tril(jnp.ones((Q, Q), dtype=bool))[:, :, None, None]
    decay = jnp.exp(jnp.where(mask, seg, -jnp.inf))
    cb = jnp.einsum('bctgn,bcsgn->bctsg', Cc, Bc)
    y_diag = jnp.einsum('bctsgr,bcsgrp->bctgrp', cb[..., None] * decay, xdt)
    decay_to_end = jnp.exp(a_cs[:, :, -1:] - a_cs)
    states = jnp.einsum('bcsgn,bcsgrp->bcgrpn', Bc, xdt * decay_to_end[..., None])
    chunk_decay = jnp.exp(a_cs[:, :, -1])

    def step(h, inp):
        s, d = inp
        return d[..., None, None] * h + s, h

    h0 = jnp.zeros((b, G, R, P, N), dtype=x.dtype)
    _, h_in = lax.scan(step, h0, (jnp.moveaxis(states, 1, 0), jnp.moveaxis(chunk_decay, 1, 0)))
    h_in = jnp.moveaxis(h_in, 0, 1)
    y_off = jnp.einsum('bctgn,bcgrpn->bctgrp', Cc, h_in) * jnp.exp(a_cs)[..., None]
    return (y_diag + y_off).reshape(b, L, H, P)


def _ssd_mixer(x, in_w, conv_w, conv_b, dt_bias, a_log, d_skip, norm_w, out_w):
    b, L, _ = x.shape
    G, N, H, P = SSM_GROUPS, SSM_STATE, SSM_HEADS, SSM_HEAD_DIM
    R = H // G
    proj = x @ in_w
    z = proj[..., :D_INNER]
    xbc = proj[..., D_INNER:D_INNER + SSM_CONV_DIM]
    dt_raw = proj[..., D_INNER + SSM_CONV_DIM:]
    xbc = jax.nn.silu(_dwconv(xbc, conv_w, conv_b)).astype(jnp.float32)
    xs = xbc[..., :D_INNER].reshape(b, L, H, P)
    Bm = xbc[..., D_INNER:D_INNER + G * N].reshape(b, L, G, N)
    Cm = xbc[..., D_INNER + G * N:].reshape(b, L, G, N)
    dt = jax.nn.softplus(dt_raw.astype(jnp.float32).reshape(b, L, 2, H) + dt_bias.astype(jnp.float32))
    A = -jnp.exp(a_log.astype(jnp.float32))
    dt_f, dt_b = dt[:, :, 0], dt[:, :, 1]
    y_f = _ssd_scan(xs, dt_f, A[0], Bm, Cm)
    y_b = jnp.flip(_ssd_scan(jnp.flip(xs, 1), jnp.flip(dt_b, 1), A[1],
                             jnp.flip(Bm, 1), jnp.flip(Cm, 1)), 1)
    cb_self = jnp.einsum('blgn,blgn->blg', Cm, Bm)
    diag = (cb_self[..., None, None] * (xs * dt_b[..., None]).reshape(b, L, G, R, P)).reshape(b, L, H, P)
    y = y_f + y_b - diag + d_skip.astype(jnp.float32)[:, None] * xs
    y = y.reshape(b, L, D_INNER) * jax.nn.silu(z.astype(jnp.float32))
    yg = y.reshape(b, L, G, D_INNER // G)
    yg = yg * lax.rsqrt(jnp.mean(yg * yg, axis=-1, keepdims=True) + EPS)
    y = (yg.reshape(b, L, D_INNER) * norm_w.astype(jnp.float32)).astype(x.dtype)
    return y @ out_w


def _conformer_conv(x, pw1_w, pw1_b, dw_w, dw_b, ln_w, ln_b, pw2_w, pw2_b):
    h = x @ pw1_w + pw1_b
    h = h[..., :CF_DIM] * jax.nn.sigmoid(h[..., CF_DIM:])
    h = _dwconv(h, dw_w, dw_b)
    h = jax.nn.silu(_layernorm(h, ln_w, ln_b))
    return h @ pw2_w + pw2_b


def _conv_ffn(x, up_w, conv_w, conv_b, down_w):
    u = _dwconv(x @ up_w, conv_w, conv_b)
    h = jax.nn.silu(u[..., :FFN_DIM]) * u[..., FFN_DIM:]
    return h @ down_w


def setup_inputs(seed: int = 0) -> dict:
    key = jax.random.key(seed)
    k = jax.random.split(key, 26)
    ns = (DEPTH + 1) // 2
    nc = DEPTH // 2

    def nrm(kk, shape, scale):
        return jax.random.normal(kk, shape, jnp.float32) * scale

    dt0 = jnp.exp(jax.random.uniform(k[5], (ns, 2, SSM_HEADS), jnp.float32,
                                     minval=math.log(1e-3), maxval=math.log(1e-1)))
    return {
        "x_prompt": nrm(k[0], (BATCH, SEQ, D_MODEL), 1.0),
        "x_sample": nrm(k[1], (DEC_BATCH, DEC_SEQ, D_MODEL), 1.0),
        "ssm_in_w": nrm(k[2], (ns, D_MODEL, SSM_IN_DIM), D_MODEL ** -0.5),
        "ssm_conv_w": nrm(k[3], (ns, SSM_CONV, SSM_CONV_DIM), SSM_CONV ** -0.5),
        "ssm_conv_b": nrm(k[4], (ns, SSM_CONV_DIM), 0.02),
        "ssm_dt_bias": dt0 + jnp.log(-jnp.expm1(-dt0)),
        "ssm_a_log": jnp.log(jax.random.uniform(k[6], (ns, 2, SSM_HEADS), jnp.float32, minval=1.0, maxval=16.0)),
        "ssm_d": 1.0 + nrm(k[7], (ns, SSM_HEADS), 0.1),
        "ssm_norm_w": 1.0 + nrm(k[8], (ns, D_INNER), 0.1),
        "ssm_out_w": nrm(k[9], (ns, D_INNER, D_MODEL), D_INNER ** -0.5),
        "cf_pw1_w": nrm(k[10], (nc, D_MODEL, 2 * CF_DIM), D_MODEL ** -0.5),
        "cf_pw1_b": nrm(k[11], (nc, 2 * CF_DIM), 0.02),
        "cf_dw_w": nrm(k[12], (nc, CF_CONV, CF_DIM), CF_CONV ** -0.5),
        "cf_dw_b": nrm(k[13], (nc, CF_DIM), 0.02),
        "cf_ln_w": 1.0 + nrm(k[14], (nc, CF_DIM), 0.1),
        "cf_ln_b": nrm(k[15], (nc, CF_DIM), 0.02),
        "cf_pw2_w": nrm(k[16], (nc, CF_DIM, D_MODEL), CF_DIM ** -0.5),
        "cf_pw2_b": nrm(k[17], (nc, D_MODEL), 0.02),
        "ffn_up_w": nrm(k[18], (DEPTH, D_MODEL, 2 * FFN_DIM), D_MODEL ** -0.5),
        "ffn_conv_w": nrm(k[19], (DEPTH, FFN_CONV, 2 * FFN_DIM), FFN_CONV ** -0.5),
        "ffn_conv_b": nrm(k[20], (DEPTH, 2 * FFN_DIM), 0.02),
        "ffn_down_w": nrm(k[21], (DEPTH, FFN_DIM, D_MODEL), FFN_DIM ** -0.5),
        "norm_pre_mix": 1.0 + nrm(k[22], (DEPTH, D_MODEL), 0.1),
        "norm_post_mix": 1.0 + nrm(k[23], (DEPTH, D_MODEL), 0.1),
        "norm_pre_ffn": 1.0 + nrm(k[24], (DEPTH, D_MODEL), 0.1),
        "norm_post_ffn": 1.0 + nrm(k[25], (DEPTH, D_MODEL), 0.1),
    }


def reference(x_prompt, x_sample, ssm_in_w, ssm_conv_w, ssm_conv_b, ssm_dt_bias, ssm_a_log,
              ssm_d, ssm_norm_w, ssm_out_w, cf_pw1_w, cf_pw1_b, cf_dw_w, cf_dw_b, cf_ln_w,
              cf_ln_b, cf_pw2_w, cf_pw2_b, ffn_up_w, ffn_conv_w, ffn_conv_b, ffn_down_w,
              norm_pre_mix, norm_post_mix, norm_pre_ffn, norm_post_ffn):
    def layer_stack(x):
        for i in range(DEPTH):
            j = i // N_MIXERS
            h = _rmsnorm(x, norm_pre_mix[i])
            if i % N_MIXERS == 0:
                h = _ssd_mixer(h, ssm_in_w[j], ssm_conv_w[j], ssm_conv_b[j], ssm_dt_bias[j],
                               ssm_a_log[j], ssm_d[j], ssm_norm_w[j], ssm_out_w[j])
            else:
                h = _conformer_conv(h, cf_pw1_w[j], cf_pw1_b[j], cf_dw_w[j], cf_dw_b[j],
                                    cf_ln_w[j], cf_ln_b[j], cf_pw2_w[j], cf_pw2_b[j])
            x = x + _rmsnorm(h, norm_post_mix[i])
            h = _conv_ffn(_rmsnorm(x, norm_pre_ffn[i]), ffn_up_w[i], ffn_conv_w[i],
                          ffn_conv_b[i], ffn_down_w[i])
            x = x + _rmsnorm(h, norm_post_ffn[i])
        return x

    y_prompt = layer_stack(x_prompt)
    y_sample = layer_stack(x_sample)
    return (y_prompt, y_sample)
```

```python
import functools

import numpy as np
import jax
import jax.numpy as jnp
from jax import lax
from jax.experimental import pallas as pl
from jax.experimental.pallas import tpu as pltpu

F32 = jnp.float32
BF = jnp.bfloat16

D = 1024
DI = 2048
G = 8
NS = 128
NH = 32
HP = 64
HR = NH // G
GW = HR * HP
Q = 128
FF = 2816
FT = 256
NF = FF // FT
CFK = 31
EPS = 1e-6
HALO = 16
VMEM_LIMIT = 56 * 1024 * 1024


def _rms(x, w):
    return x * lax.rsqrt(jnp.mean(x * x, axis=-1, keepdims=True) + EPS) * w


def _sigmoid(x):
    return 1.0 / (1.0 + jnp.exp(-x))


def _silu(x):
    return x * _sigmoid(x)


def _softplus(x):
    return jnp.maximum(x, 0.0) + jnp.log1p(jnp.exp(-jnp.abs(x)))


def _split3(v):
    hi = v.astype(BF)
    r1 = v - hi.astype(F32)
    mid = r1.astype(BF)
    lo = (r1 - mid.astype(F32)).astype(BF)
    return jnp.concatenate([hi, mid, lo], axis=0)


def _const_spec(shape):
    nd = len(shape)
    return pl.BlockSpec(shape, lambda *_: (0,) * nd, pipeline_mode=pl.Buffered(1))


def _tile_specs(TL, L, width):
    r = TL // HALO
    last = L // HALO - 1
    main = pl.BlockSpec((1, TL, width), lambda b, i: (b, i, 0))
    prev = pl.BlockSpec((1, HALO, width), lambda b, i: (b, jnp.maximum(i * r - 1, 0), 0))
    nxt = pl.BlockSpec((1, HALO, width), lambda b, i: (b, jnp.minimum((i + 1) * r, last), 0))
    return main, prev, nxt


def _edge_flags():
    i = pl.program_id(1)
    nt = pl.num_programs(1)
    keep_prev = jnp.where(i > 0, jnp.float32(1), jnp.float32(0))
    keep_next = jnp.where(i < nt - 1, jnp.float32(1), jnp.float32(0))
    return keep_prev, keep_next


def _ssd_in_kernel(x_ref, xp_ref, xq_ref, nw_ref, wz_ref, wx_ref, wdt_ref, cw_ref, cb_ref,
                   dtb_ref, alog_ref,
                   z_ref, xs_ref, b_ref, c_ref, pt_ref,
                   lhs_ref, u_ref, *, TL):
    keep_prev, keep_next = _edge_flags()
    nw = nw_ref[...]
    lhs_ref[0:HALO, :] = (_rms(xp_ref[0], nw) * keep_prev).astype(BF)
    lhs_ref[HALO:HALO + TL, :] = _rms(x_ref[0], nw).astype(BF)
    lhs_ref[HALO + TL:2 * HALO + TL, :] = (_rms(xq_ref[0], nw) * keep_next).astype(BF)
    xm = lhs_ref[HALO:HALO + TL, :]

    for g in range(G):
        z_ref[0, g] = jnp.dot(xm, wz_ref[g], preferred_element_type=F32).astype(BF)

    for j in range(2 * G):
        us = u_ref.at[j % 2]
        us[...] = jnp.dot(lhs_ref[...], wx_ref[j], preferred_element_type=F32)
        cw = cw_ref[j]
        v = (us[HALO - 1:HALO - 1 + TL, :] * cw[0:1, :]
             + us[HALO:HALO + TL, :] * cw[1:2, :]
             + us[HALO + 1:HALO + 1 + TL, :] * cw[2:3, :]
             + cb_ref[j])
        v = _silu(v).astype(BF)
        if j < G:
            xs_ref[0, j] = v
        elif j < G + G // 2:
            b_ref[0, 2 * (j - G)] = v[:, :NS]
            b_ref[0, 2 * (j - G) + 1] = v[:, NS:]
        else:
            c_ref[0, 2 * (j - G - G // 2)] = v[:, :NS]
            c_ref[0, 2 * (j - G - G // 2) + 1] = v[:, NS:]

    dtr = lax.dot_general(wdt_ref[...], xm, (((1,), (1,)), ((), ())), preferred_element_type=F32)
    dt = _softplus(dtr + dtb_ref[...])
    da = dt * (-jnp.exp(alog_ref[...]))
    lane = lax.broadcasted_iota(jnp.int32, (2 * NH, TL), 1) & (Q - 1)
    cs = da
    sh = 1
    while sh < Q:
        cs = cs + jnp.where(lane >= sh, pltpu.roll(cs, sh, axis=1), 0.0)
        sh *= 2
    tot = jnp.concatenate(
        [jnp.broadcast_to(cs[:, c * Q + Q - 1:c * Q + Q], (2 * NH, Q)) for c in range(TL // Q)], axis=1)
    row = lax.broadcasted_iota(jnp.int32, (2 * NH, TL), 0)
    isf = (row & 7) < HR
    ex = cs - da
    q0 = jnp.where(isf, cs, ex)
    q2 = jnp.exp(jnp.where(isf, cs, tot - ex))
    q3 = dt * jnp.exp(jnp.where(isf, tot - cs, ex))
    for g in range(G):
        pt_ref[0, g, 0:8, :] = q0[8 * g:8 * g + 8, :]
        pt_ref[0, g, 8:16, :] = dt[8 * g:8 * g + 8, :]
        pt_ref[0, g, 16:24, :] = q2[8 * g:8 * g + 8, :]
        pt_ref[0, g, 24:32, :] = q3[8 * g:8 * g + 8, :]


def _ssd_in(x, nw, wz, wx, wdt, cw, cb, dtb, alog, *, TL):
    Bt, L, _ = x.shape
    main, prev, nxt = _tile_specs(TL, L, D)
    gspec = lambda w: pl.BlockSpec((1, G, TL, w), lambda b, i: (b, 0, i, 0))
    return pl.pallas_call(
        functools.partial(_ssd_in_kernel, TL=TL),
        grid=(Bt, L // TL),
        in_specs=[main, prev, nxt, _const_spec((1, D)), _const_spec((G, D, GW)),
                  _const_spec((2 * G, D, GW)), _const_spec((2 * NH, D)),
                  _const_spec((2 * G, 3, GW)), _const_spec((2 * G, 1, GW)),
                  _const_spec((2 * NH, TL)), _const_spec((2 * NH, TL))],
        out_specs=[gspec(GW), gspec(GW), gspec(NS), gspec(NS),
                   pl.BlockSpec((1, G, 32, TL), lambda b, i: (b, 0, 0, i))],
        out_shape=[jax.ShapeDtypeStruct((Bt, G, L, GW), BF), jax.ShapeDtypeStruct((Bt, G, L, GW), BF),
                   jax.ShapeDtypeStruct((Bt, G, L, NS), BF), jax.ShapeDtypeStruct((Bt, G, L, NS), BF),
                   jax.ShapeDtypeStruct((Bt, G, 32, L), F32)],
        scratch_shapes=[pltpu.VMEM((TL + 2 * HALO, D), BF), pltpu.VMEM((2, TL + 2 * HALO, GW), F32)],
        compiler_params=pltpu.CompilerParams(
            dimension_semantics=("arbitrary", "arbitrary"), vmem_limit_bytes=VMEM_LIMIT),
        name="ssd_in",
    )(x, x, x, nw, wz, wx, wdt, cw, cb, dtb, alog)


def _expansion_matrix():
    sel = np.zeros((96, 2048), np.float32)
    for piece in range(3):
        for j in range(8):
            sel[piece * 32 + j, 128 * j:128 * j + 128] = 1.0
        for q in range(4):
            for r in range(HR):
                c0 = 1024 + 256 * q + HP * r
                sel[piece * 32 + 16 + 4 * q + r, c0:c0 + HP] = 1.0
    return sel


def _tn_dot(a, b):
    return lax.dot_general(a, b, (((0,), (0,)), ((), ())), preferred_element_type=F32)


def _scan_kernel(xs_ref, b_ref, c_ref, z_ref, pt_ref, sel_ref, selb_ref, dsk_ref, gnw_ref,
                 y_ref, hf_ref, hb_ref, snap_ref, ex_ref, s_ref, hin_ref, *, TB):
    p = pl.program_id(1)
    i = pl.program_id(2)
    nb = pl.num_programs(2)
    NC = TB // Q

    @pl.when(jnp.logical_and(p == 0, i == 0))
    def _():
        hb_ref[...] = jnp.zeros(hb_ref.shape, F32)

    @pl.when(jnp.logical_and(p == 1, i == 0))
    def _():
        hf_ref[...] = jnp.zeros(hf_ref.shape, F32)

    @pl.when(p == 0)
    def _():
        j = nb - 1 - i

        def gbody(g, carry):
            h = hb_ref[g]
            snap_ref[j, g] = h
            for c in reversed(range(NC)):
                sl = slice(c * Q, (c + 1) * Q)
                e = _tn_dot(_split3(pt_ref[0, g, :, sl]), selb_ref[...])
                xw = (xs_ref[0, g, sl, :].astype(F32) * e[:, GW:]).astype(BF)
                h = e[0:1, :GW] * h + _tn_dot(b_ref[0, g, sl, :], xw)
            hb_ref[g] = h
            return carry

        lax.fori_loop(0, G, gbody, 0)

    @pl.when(p == 1)
    def _():
        tt = lax.broadcasted_iota(jnp.int32, (Q, Q), 0)
        ss = lax.broadcasted_iota(jnp.int32, (Q, Q), 1)
        lower = ss <= tt
        head_of_lane = lax.broadcasted_iota(jnp.int32, (Q, GW), 1) >> 6

        def gbody(g, carry):
            for c in range(NC):
                sl = slice(c * Q, (c + 1) * Q)
                ex_ref[c] = _tn_dot(_split3(pt_ref[0, g, :, sl]), sel_ref[...])
                x32 = xs_ref[0, g, sl, :].astype(F32)
                xw = jnp.concatenate([(x32 * ex_ref[c, :, 1536:1792]).astype(BF),
                                      (x32 * ex_ref[c, :, 1792:2048]).astype(BF)], axis=1)
                s_ref[c] = _tn_dot(b_ref[0, g, sl, :], xw)
            h = snap_ref[i, g]
            for c in reversed(range(NC)):
                hin_ref[c, :, GW:] = h.astype(BF)
                h = ex_ref[c, 0:1, 1280:1536] * h + s_ref[c, :, GW:]
            h = hf_ref[g]
            for c in range(NC):
                hin_ref[c, :, :GW] = h.astype(BF)
                h = ex_ref[c, Q - 1:Q, 1024:1280] * h + s_ref[c, :, :GW]
            hf_ref[g] = h
            for c in range(NC):
                sl = slice(c * Q, (c + 1) * Q)
                xb = xs_ref[0, g, sl, :]
                cm = c_ref[0, g, sl, :]
                cbm = lax.dot_general(cm, b_ref[0, g, sl, :], (((1,), (1,)), ((), ())),
                                      preferred_element_type=F32)
                ms = []
                xblk = []
                for r in range(HR):
                    pt_f = pt_ref[0, g, r:r + 1, sl]
                    pt_b = pt_ref[0, g, HR + r:HR + r + 1, sl]
                    dt_f = pt_ref[0, g, 8 + r:9 + r, sl]
                    dt_b = pt_ref[0, g, 12 + r:13 + r, sl]
                    zf = ex_ref[c, :, r * Q:(r + 1) * Q] - pt_f
                    zb = pt_b - ex_ref[c, :, (HR + r) * Q:(HR + r + 1) * Q]
                    w = jnp.exp(jnp.where(lower, zf, zb)) * jnp.where(lower, dt_f, dt_b)
                    ms.append((cbm * w).astype(BF))
                    xblk.append(jnp.where(head_of_lane == r, xb, jnp.zeros_like(xb)))
                y = jnp.dot(jnp.concatenate(ms, axis=1), jnp.concatenate(xblk, axis=0),
                            preferred_element_type=F32)
                yoff = jnp.dot(cm, hin_ref[c], preferred_element_type=F32)
                y = (y + ex_ref[c, :, 1024:1280] * yoff[:, :GW] + ex_ref[c, :, 1280:1536] * yoff[:, GW:]
                     + dsk_ref[g] * xb.astype(F32))
                y = y * _silu(z_ref[0, g, sl, :].astype(F32))
                y = y * lax.rsqrt(jnp.mean(y * y, axis=-1, keepdims=True) + EPS) * gnw_ref[g]
                y_ref[0, g, sl, :] = y.astype(BF)
            return carry

        lax.fori_loop(0, G, gbody, 0)


def _ssd_scan(xs, bm, cm, z, pt, sel, selb, dsk, gnw, *, TB):
    Bt, _, L, _ = xs.shape
    nb = L // TB

    def both(w):
        return pl.BlockSpec((1, G, TB, w), lambda b, p, i: (b, 0, jnp.where(p == 0, nb - 1 - i, i), 0))

    def fwd(w):
        return pl.BlockSpec((1, G, TB, w), lambda b, p, i: (b, 0, p * i, 0))

    pt_spec = pl.BlockSpec((1, G, 32, TB), lambda b, p, i: (b, 0, 0, jnp.where(p == 0, nb - 1 - i, i)))
    return pl.pallas_call(
        functools.partial(_scan_kernel, TB=TB),
        grid=(Bt, 2, nb),
        in_specs=[both(GW), both(NS), fwd(NS), fwd(GW), pt_spec,
                  _const_spec((96, 2048)), _const_spec((96, 2 * GW)),
                  _const_spec((G, 1, GW)), _const_spec((G, 1, GW))],
        out_specs=fwd(GW),
        out_shape=jax.ShapeDtypeStruct((Bt, G, L, GW), BF),
        scratch_shapes=[pltpu.VMEM((G, NS, GW), F32), pltpu.VMEM((G, NS, GW), F32),
                        pltpu.VMEM((nb, G, NS, GW), F32),
                        pltpu.VMEM((TB // Q, Q, 2048), F32), pltpu.VMEM((TB // Q, NS, 2 * GW), F32),
                        pltpu.VMEM((TB // Q, NS, 2 * GW), BF)],
        compiler_params=pltpu.CompilerParams(
            dimension_semantics=("arbitrary", "arbitrary", "arbitrary"), vmem_limit_bytes=VMEM_LIMIT),
        name="ssd_scan",
    )(xs, bm, cm, z, pt, sel, selb, dsk, gnw)


def _conv_ffn(x1w, keep_prev, keep_next, nw_pre, up_ref, fcw_ref, fcb_ref, down_ref, nw_post,
              flhs_ref, fu_ref, fh_ref, *, TL):
    TW = TL + 16
    rowid = lax.broadcasted_iota(jnp.int32, (TW, 1), 0)
    keep = jnp.where(rowid < 8, keep_prev, jnp.where(rowid >= TL + 8, keep_next, 1.0))
    flhs_ref[...] = (_rms(x1w, nw_pre) * keep).astype(BF)
    for f in range(NF):
        us = fu_ref.at[f % 2]
        us[...] = jnp.dot(flhs_ref[...], up_ref[f], preferred_element_type=F32)
        cw = fcw_ref[f]
        v = (us[7:7 + TL, :] * cw[0:1, :] + us[8:8 + TL, :] * cw[1:2, :]
             + us[9:9 + TL, :] * cw[2:3, :] + fcb_ref[f])
        fh_ref[:, f * FT:(f + 1) * FT] = (_silu(v[:, :FT]) * v[:, FT:]).astype(BF)
    o = jnp.dot(fh_ref[...], down_ref[...], preferred_element_type=F32)
    return x1w[8:8 + TL, :] + _rms(o, nw_post)


_FFN_SCRATCH = lambda TL: [pltpu.VMEM((TL + 16, D), BF), pltpu.VMEM((2, TL + 16, 2 * FT), F32),
                           pltpu.VMEM((TL, FF), BF)]
_FFN_SPECS = lambda: [_const_spec((1, D)), _const_spec((NF, D, 2 * FT)), _const_spec((NF, 3, 2 * FT)),
                      _const_spec((NF, 1, 2 * FT)), _const_spec((FF, D)), _const_spec((1, D))]


def _ssd_out_kernel(x_ref, xp_ref, xq_ref, y_ref, yp_ref, yq_ref, ow_ref, pw_ref,
                    npre_ref, up_ref, fcw_ref, fcb_ref, down_ref, npost_ref,
                    o_ref, ylhs_ref, flhs_ref, fu_ref, fh_ref, *, TL):
    keep_prev, keep_next = _edge_flags()
    for g in range(G):
        ylhs_ref[0:HALO, g * GW:(g + 1) * GW] = yp_ref[0, g]
        ylhs_ref[HALO:HALO + TL, g * GW:(g + 1) * GW] = y_ref[0, g]
        ylhs_ref[HALO + TL:2 * HALO + TL, g * GW:(g + 1) * GW] = yq_ref[0, g]
    h = jnp.dot(ylhs_ref[...], ow_ref[...], preferred_element_type=F32)[8:TL + 24, :]
    xw = jnp.concatenate([xp_ref[0, 8:16, :], x_ref[0], xq_ref[0, 0:8, :]], axis=0)
    x1w = xw + _rms(h, pw_ref[...])
    o_ref[0] = _conv_ffn(x1w, keep_prev, keep_next, npre_ref[...], up_ref, fcw_ref, fcb_ref, down_ref,
                         npost_ref[...], flhs_ref, fu_ref, fh_ref, TL=TL)


def _ssd_out(x, y, ow, pw, npre, up, fcw, fcb, down, npost, *, TL):
    Bt, L, _ = x.shape
    main, prev, nxt = _tile_specs(TL, L, D)
    r = TL // HALO
    last = L // HALO - 1
    ymain = pl.BlockSpec((1, G, TL, GW), lambda b, i: (b, 0, i, 0))
    yprev = pl.BlockSpec((1, G, HALO, GW), lambda b, i: (b, 0, jnp.maximum(i * r - 1, 0), 0))
    ynext = pl.BlockSpec((1, G, HALO, GW), lambda b, i: (b, 0, jnp.minimum((i + 1) * r, last), 0))
    return pl.pallas_call(
        functools.partial(_ssd_out_kernel, TL=TL),
        grid=(Bt, L // TL),
        in_specs=[main, prev, nxt, ymain, yprev, ynext, _const_spec((DI, D)), _const_spec((1, D))]
                 + _FFN_SPECS(),
        out_specs=main,
        out_shape=jax.ShapeDtypeStruct((Bt, L, D), F32),
        scratch_shapes=[pltpu.VMEM((TL + 2 * HALO, DI), BF)] + _FFN_SCRATCH(TL),
        compiler_params=pltpu.CompilerParams(
            dimension_semantics=("arbitrary", "arbitrary"), vmem_limit_bytes=VMEM_LIMIT),
        name="ssd_out",
    )(x, x, x, y, y, y, ow, pw, npre, up, fcw, fcb, down, npost)


CONV_ROWS = 24


def _conformer_kernel(x_ref, xp_ref, xq_ref, nmix_ref, w1_ref, b1_ref, dw_ref, db_ref, lnw_ref, lnb_ref,
                      w2_ref, b2_ref, pw_ref, npre_ref, up_ref, fcw_ref, fcb_ref, down_ref, npost_ref,
                      o_ref, lhs_ref, g_ref, cv_ref, l2_ref, flhs_ref, fu_ref, fh_ref, *, TL):
    keep_prev, keep_next = _edge_flags()
    TA = TL + 2 * HALO
    TW = TL + 16
    nmix = nmix_ref[...]
    lhs_ref[0:HALO, :] = _rms(xp_ref[0], nmix).astype(BF)
    lhs_ref[HALO:HALO + TL, :] = _rms(x_ref[0], nmix).astype(BF)
    lhs_ref[HALO + TL:TA, :] = _rms(xq_ref[0], nmix).astype(BF)
    rowid = lax.broadcasted_iota(jnp.int32, (TA, 1), 0)
    keep = jnp.where(rowid < HALO, keep_prev, jnp.where(rowid >= HALO + TL, keep_next, 1.0))
    half = D // 2
    for hb in range(2):
        cs = slice(hb * half, (hb + 1) * half)
        a = jnp.dot(lhs_ref[...], w1_ref[:, cs], preferred_element_type=F32) + b1_ref[:, cs]
        gt = jnp.dot(lhs_ref[...], w1_ref[:, D + hb * half:D + (hb + 1) * half],
                     preferred_element_type=F32) + b1_ref[:, D + hb * half:D + (hb + 1) * half]
        g_ref[8:8 + TA, cs] = a * _sigmoid(gt) * keep
    g_ref[0:8, :] = jnp.zeros((8, D), F32)
    g_ref[8 + TA:16 + TA, :] = jnp.zeros((8, D), F32)
    for r0 in range(0, TW, CONV_ROWS):
        for lb in range(D // 128):
            ls = slice(lb * 128, (lb + 1) * 128)
            acc = g_ref[r0 + 1:r0 + 1 + CONV_ROWS, ls] * dw_ref[0:1, ls]
            for k in range(1, CFK):
                acc = acc + g_ref[r0 + 1 + k:r0 + 1 + k + CONV_ROWS, ls] * dw_ref[k:k + 1, ls]
            cv_ref[r0:r0 + CONV_ROWS, ls] = acc
    cv = cv_ref[...] + db_ref[...]
    mu = jnp.mean(cv, axis=-1, keepdims=True)
    xc = cv - mu
    ln = xc * lax.rsqrt(jnp.mean(xc * xc, axis=-1, keepdims=True) + EPS) * lnw_ref[...] + lnb_ref[...]
    l2_ref[...] = _silu(ln).astype(BF)
    h = jnp.dot(l2_ref[...], w2_ref[...], preferred_element_type=F32) + b2_ref[...]
    xw = jnp.concatenate([xp_ref[0, 8:16, :], x_ref[0], xq_ref[0, 0:8, :]], axis=0)
    x1w = xw + _rms(h, pw_ref[...])
    o_ref[0] = _conv_ffn(x1w, keep_prev, keep_next, npre_ref[...], up_ref, fcw_ref, fcb_ref, down_ref,
                         npost_ref[...], flhs_ref, fu_ref, fh_ref, TL=TL)


def _conformer(x, nmix, w1, b1, dw, db, lnw, lnb, w2, b2, pw, npre, up, fcw, fcb, down, npost, *, TL):
    Bt, L, _ = x.shape
    assert (TL + 16) % CONV_ROWS == 0
    main, prev, nxt = _tile_specs(TL, L, D)
    return pl.pallas_call(
        functools.partial(_conformer_kernel, TL=TL),
        grid=(Bt, L // TL),
        in_specs=[main, prev, nxt, _const_spec((1, D)), _const_spec((D, 2 * D)), _const_spec((1, 2 * D)),
                  _const_spec((CFK, D)), _const_spec((1, D)), _const_spec((1, D)), _const_spec((1, D)),
                  _const_spec((D, D)), _const_spec((1, D)), _const_spec((1, D))] + _FFN_SPECS(),
        out_specs=main,
        out_shape=jax.ShapeDtypeStruct((Bt, L, D), F32),
        scratch_shapes=[pltpu.VMEM((TL + 2 * HALO, D), BF), pltpu.VMEM((TL + 2 * HALO + 16, D), F32),
                        pltpu.VMEM((TL + 16, D), F32), pltpu.VMEM((TL + 16, D), BF)] + _FFN_SCRATCH(TL),
        compiler_params=pltpu.CompilerParams(
            dimension_semantics=("arbitrary", "arbitrary"), vmem_limit_bytes=VMEM_LIMIT),
        name="conformer",
    )(x, x, x, nmix, w1, b1, dw, db, lnw, lnb, w2, b2, pw, npre, up, fcw, fcb, down, npost)


def _ffn_params(i, ffn_up_w, ffn_conv_w, ffn_conv_b, ffn_down_w, norm_pre_ffn, norm_post_ffn):
    def pair(a):
        lead = a.shape[:-1]
        a = a.reshape(lead + (2, NF, FT))
        a = jnp.moveaxis(a, -2, 0)
        return a.reshape((NF,) + lead + (2 * FT,))
    return (norm_pre_ffn[i][None, :], pair(ffn_up_w[i]).astype(BF), pair(ffn_conv_w[i]),
            pair(ffn_conv_b[i][None, :]), ffn_down_w[i].astype(BF), norm_post_ffn[i][None, :])


def kernel(x_prompt, x_sample, ssm_in_w, ssm_conv_w, ssm_conv_b, ssm_dt_bias, ssm_a_log, ssm_d, ssm_norm_w, ssm_out_w, cf_pw1_w, cf_pw1_b, cf_dw_w, cf_dw_b, cf_ln_w, cf_ln_b, cf_pw2_w, cf_pw2_b, ffn_up_w, ffn_conv_w, ffn_conv_b, ffn_down_w, norm_pre_mix, norm_post_mix, norm_pre_ffn, norm_post_ffn):
    TL = 512
    TB = 512
    in_w = ssm_in_w[0]
    wz = in_w[:, :DI].reshape(D, G, GW).transpose(1, 0, 2).astype(BF)
    wx = in_w[:, DI:DI + 2 * DI].reshape(D, 2 * G, GW).transpose(1, 0, 2).astype(BF)
    perm = np.array([d * NH + HR * g + r for g in range(G) for d in range(2) for r in range(HR)])
    wdt = in_w[:, DI + 2 * DI:][:, perm].T.astype(BF)
    dtb = jnp.broadcast_to(ssm_dt_bias[0].reshape(-1)[perm][:, None], (2 * NH, TL))
    alog = jnp.broadcast_to(ssm_a_log[0].reshape(-1)[perm][:, None], (2 * NH, TL))
    cw = ssm_conv_w[0].reshape(3, 2 * G, GW).transpose(1, 0, 2)
    cb = ssm_conv_b[0].reshape(2 * G, 1, GW)
    sel_np = _expansion_matrix()
    sel = jnp.asarray(sel_np, BF)
    selb = jnp.asarray(np.concatenate([sel_np[:, 1280:1536], sel_np[:, 1792:2048]], axis=1), BF)
    dsk = jnp.repeat(ssm_d[0], HP).reshape(G, 1, GW)
    gnw = ssm_norm_w[0].reshape(G, 1, GW)
    ow = ssm_out_w[0].astype(BF)
    ffn0 = _ffn_params(0, ffn_up_w, ffn_conv_w, ffn_conv_b, ffn_down_w, norm_pre_ffn, norm_post_ffn)
    ffn1 = _ffn_params(1, ffn_up_w, ffn_conv_w, ffn_conv_b, ffn_down_w, norm_pre_ffn, norm_post_ffn)
    w1 = cf_pw1_w[0].astype(BF)
    w2 = cf_pw2_w[0].astype(BF)
    row = lambda a: a[None, :]

    def layer_stack(x):
        z, xs, bm, cm, pt = _ssd_in(x, row(norm_pre_mix[0]), wz, wx, wdt, cw, cb, dtb, alog, TL=TL)
        y = _ssd_scan(xs, bm, cm, z, pt, sel, selb, dsk, gnw, TB=TB)
        x = _ssd_out(x, y, ow, row(norm_post_mix[0]), *ffn0, TL=TL)
        x = _conformer(x, row(norm_pre_mix[1]), w1, row(cf_pw1_b[0]), cf_dw_w[0], row(cf_dw_b[0]),
                       row(cf_ln_w[0]), row(cf_ln_b[0]), w2, row(cf_pw2_b[0]), row(norm_post_mix[1]),
                       *ffn1, TL=TL)
        return x

    return (layer_stack(x_prompt), layer_stack(x_sample))
```

```python
import functools

import numpy as np
import jax
import jax.numpy as jnp
from jax import lax
from jax.experimental import pallas as pl
from jax.experimental.pallas import tpu as pltpu

F32 = jnp.float32
BF = jnp.bfloat16

D = 1024
DI = 2048
G = 8
NS = 128
NH = 32
HP = 64
HR = NH // G
GW = HR * HP
Q = 128
FF = 2816
FT = 256
NF = FF // FT
CFK = 31
EPS = 1e-6
HALO = 16
VMEM_LIMIT = 56 * 1024 * 1024


def _rms(x, w):
    return x * lax.rsqrt(jnp.mean(x * x, axis=-1, keepdims=True) + EPS) * w


def _sigmoid(x):
    return 1.0 / (1.0 + jnp.exp(-x))


def _silu(x):
    return x * _sigmoid(x)


def _softplus(x):
    return jnp.maximum(x, 0.0) + jnp.log1p(jnp.exp(-jnp.abs(x)))


def _split3(v):
    hi = v.astype(BF)
    r1 = v - hi.astype(F32)
    mid = r1.astype(BF)
    lo = (r1 - mid.astype(F32)).astype(BF)
    return jnp.concatenate([hi, mid, lo], axis=0)


def _const_spec(shape):
    nd = len(shape)
    return pl.BlockSpec(shape, lambda *_: (0,) * nd, pipeline_mode=pl.Buffered(1))


def _tile_specs(TL, L, width):
    r = TL // HALO
    last = L // HALO - 1
    main = pl.BlockSpec((1, TL, width), lambda b, i: (b, i, 0))
    prev = pl.BlockSpec((1, HALO, width), lambda b, i: (b, jnp.maximum(i * r - 1, 0), 0))
    nxt = pl.BlockSpec((1, HALO, width), lambda b, i: (b, jnp.minimum((i + 1) * r, last), 0))
    return main, prev, nxt


def _edge_flags():
    i = pl.program_id(1)
    nt = pl.num_programs(1)
    keep_prev = jnp.where(i > 0, jnp.float32(1), jnp.float32(0))
    keep_next = jnp.where(i < nt - 1, jnp.float32(1), jnp.float32(0))
    return keep_prev, keep_next


def _tn_dot(a, b):
    return lax.dot_general(a, b, (((0,), (0,)), ((), ())), preferred_element_type=F32)


def _ssd_in_kernel(x_ref, xp_ref, xq_ref, nw_ref, wz_ref, wx_ref, wdt_ref, cw_ref, cb_ref,
                   dtb_ref, alog_ref, sel4_ref,
                   z_ref, xs_ref, b_ref, c_ref, pt_ref, sf_ref, sb_ref, df_ref, db_ref,
                   lhs_ref, u_ref, e4_ref, *, TL):
    keep_prev, keep_next = _edge_flags()
    NC = TL // Q
    nw = nw_ref[...]
    lhs_ref[0:HALO, :] = (_rms(xp_ref[0], nw) * keep_prev).astype(BF)
    lhs_ref[HALO:HALO + TL, :] = _rms(x_ref[0], nw).astype(BF)
    lhs_ref[HALO + TL:2 * HALO + TL, :] = (_rms(xq_ref[0], nw) * keep_next).astype(BF)
    xm = lhs_ref[HALO:HALO + TL, :]

    for g in range(G):
        z_ref[0, g] = jnp.dot(xm, wz_ref[g], preferred_element_type=F32).astype(BF)

    dtr = lax.dot_general(wdt_ref[...], xm, (((1,), (1,)), ((), ())), preferred_element_type=F32)
    dt = _softplus(dtr + dtb_ref[...])
    da = dt * (-jnp.exp(alog_ref[...]))
    lane = lax.broadcasted_iota(jnp.int32, (2 * NH, TL), 1) & (Q - 1)
    cs = da
    sh = 1
    while sh < Q:
        cs = cs + jnp.where(lane >= sh, pltpu.roll(cs, sh, axis=1), 0.0)
        sh *= 2
    tot = jnp.concatenate(
        [jnp.broadcast_to(cs[:, c * Q + Q - 1:c * Q + Q], (2 * NH, Q)) for c in range(NC)], axis=1)
    row = lax.broadcasted_iota(jnp.int32, (2 * NH, TL), 0)
    isf = (row & 7) < HR
    ex = cs - da
    q0 = jnp.where(isf, cs, ex)
    q2 = jnp.exp(jnp.where(isf, cs, tot - ex))
    q3 = dt * jnp.exp(jnp.where(isf, tot - cs, ex))
    for g in range(G):
        pt_ref[0, g, 0:8, :] = q0[8 * g:8 * g + 8, :]
        pt_ref[0, g, 8:16, :] = dt[8 * g:8 * g + 8, :]
        pt_ref[0, g, 16:24, :] = q2[8 * g:8 * g + 8, :]
        pt_ref[0, g, 24:32, :] = q3[8 * g:8 * g + 8, :]
    df_ref[...] = jnp.zeros(df_ref.shape, F32)
    db_ref[...] = jnp.zeros(db_ref.shape, F32)

    def conv_block(j):
        us = u_ref.at[j % 2]
        us[...] = jnp.dot(lhs_ref[...], wx_ref[j], preferred_element_type=F32)
        cw = cw_ref[j]
        u = us[...]
        v = (pltpu.roll(u, 1, axis=0)[HALO:HALO + TL, :] * cw[0:1, :]
             + u[HALO:HALO + TL, :] * cw[1:2, :]
             + pltpu.roll(u, TL + 2 * HALO - 1, axis=0)[HALO:HALO + TL, :] * cw[2:3, :]
             + cb_ref[j])
        return _silu(v)

    for jj in range(G // 2):
        v = conv_block(G + jj).astype(BF)
        b_ref[0, 2 * jj] = v[:, :NS]
        b_ref[0, 2 * jj + 1] = v[:, NS:]
        v = conv_block(G + G // 2 + jj).astype(BF)
        c_ref[0, 2 * jj] = v[:, :NS]
        c_ref[0, 2 * jj + 1] = v[:, NS:]

    for g in range(G):
        v = conv_block(g)
        xs_ref[0, g] = v.astype(BF)
        gs = slice(g * GW, (g + 1) * GW)
        tab = jnp.concatenate([q2[8 * g:8 * g + 8, :], q3[8 * g:8 * g + 8, :]], axis=0)
        e4_ref[...] = _tn_dot(_split3(tab), sel4_ref[...])
        for c in range(NC):
            sl = slice(c * Q, (c + 1) * Q)
            e4 = e4_ref[sl, :]
            xv = v[sl, :]
            xw = jnp.concatenate([(xv * e4[:, 2 * GW:3 * GW]).astype(BF),
                                  (xv * e4[:, 3 * GW:4 * GW]).astype(BF)], axis=1)
            s = _tn_dot(b_ref[0, g, sl, :], xw)
            sf_ref[0, c, :, gs] = s[:, :GW].astype(BF)
            sb_ref[0, c, :, gs] = s[:, GW:].astype(BF)
            df_ref[0, 0, c:c + 1, gs] = e4[Q - 1:Q, 0:GW]
            db_ref[0, 0, c:c + 1, gs] = e4[0:1, GW:2 * GW]


def _ssd_in(x, nw, wz, wx, wdt, cw, cb, dtb, alog, sel4, *, TL):
    Bt, L, _ = x.shape
    NC = TL // Q
    main, prev, nxt = _tile_specs(TL, L, D)
    gspec = lambda w: pl.BlockSpec((1, G, TL, w), lambda b, i: (b, 0, i, 0))
    sspec = pl.BlockSpec((1, NC, NS, DI), lambda b, i: (b, i, 0, 0))
    dspec = pl.BlockSpec((1, 1, 8, DI), lambda b, i: (b, i, 0, 0))
    return pl.pallas_call(
        functools.partial(_ssd_in_kernel, TL=TL),
        grid=(Bt, L // TL),
        in_specs=[main, prev, nxt, _const_spec((1, D)), _const_spec((G, D, GW)),
                  _const_spec((2 * G, D, GW)), _const_spec((2 * NH, D)),
                  _const_spec((2 * G, 3, GW)), _const_spec((2 * G, 1, GW)),
                  _const_spec((2 * NH, TL)), _const_spec((2 * NH, TL)), _const_spec((48, 4 * GW))],
        out_specs=[gspec(GW), gspec(GW), gspec(NS), gspec(NS),
                   pl.BlockSpec((1, G, 32, TL), lambda b, i: (b, 0, 0, i)),
                   sspec, sspec, dspec, dspec],
        out_shape=[jax.ShapeDtypeStruct((Bt, G, L, GW), BF), jax.ShapeDtypeStruct((Bt, G, L, GW), BF),
                   jax.ShapeDtypeStruct((Bt, G, L, NS), BF), jax.ShapeDtypeStruct((Bt, G, L, NS), BF),
                   jax.ShapeDtypeStruct((Bt, G, 32, L), F32),
                   jax.ShapeDtypeStruct((Bt, L // Q, NS, DI), BF),
                   jax.ShapeDtypeStruct((Bt, L // Q, NS, DI), BF),
                   jax.ShapeDtypeStruct((Bt, L // TL, 8, DI), F32),
                   jax.ShapeDtypeStruct((Bt, L // TL, 8, DI), F32)],
        scratch_shapes=[pltpu.VMEM((TL + 2 * HALO, D), BF), pltpu.VMEM((2, TL + 2 * HALO, GW), F32),
                        pltpu.VMEM((TL, 4 * GW), F32)],
        compiler_params=pltpu.CompilerParams(
            dimension_semantics=("arbitrary", "arbitrary"), vmem_limit_bytes=VMEM_LIMIT),
        name="ssd_in",
    )(x, x, x, nw, wz, wx, wdt, cw, cb, dtb, alog, sel4)


def _states_kernel(sf_ref, sb_ref, df_ref, db_ref, hf_out, hb_out, hf_ref, hb_ref, *, NC):
    @pl.when(pl.program_id(1) == 0)
    def _():
        hf_ref[...] = jnp.zeros(hf_ref.shape, F32)
        hb_ref[...] = jnp.zeros(hb_ref.shape, F32)

    for g in range(G):
        gs = slice(g * GW, (g + 1) * GW)
        h = hf_ref[:, gs]
        for c in range(NC):
            hf_out[0, c, g] = h.astype(BF)
            h = df_ref[0, 0, c:c + 1, gs] * h + sf_ref[0, c, :, gs].astype(F32)
        hf_ref[:, gs] = h
        h = hb_ref[:, gs]
        for c in reversed(range(NC)):
            hb_out[0, c, g] = h.astype(BF)
            h = db_ref[0, 0, c:c + 1, gs] * h + sb_ref[0, c, :, gs].astype(F32)
        hb_ref[:, gs] = h


def _ssd_states(sf, sb, df, db, *, NC):
    Bt, nck, _, _ = sf.shape
    nb = nck // NC
    s_f = pl.BlockSpec((1, NC, NS, DI), lambda b, i: (b, i, 0, 0))
    s_b = pl.BlockSpec((1, NC, NS, DI), lambda b, i: (b, nb - 1 - i, 0, 0))
    d_f = pl.BlockSpec((1, 1, 8, DI), lambda b, i: (b, i, 0, 0))
    d_b = pl.BlockSpec((1, 1, 8, DI), lambda b, i: (b, nb - 1 - i, 0, 0))
    h_f = pl.BlockSpec((1, NC, G, NS, GW), lambda b, i: (b, i, 0, 0, 0))
    h_b = pl.BlockSpec((1, NC, G, NS, GW), lambda b, i: (b, nb - 1 - i, 0, 0, 0))
    hshape = jax.ShapeDtypeStruct((Bt, nck, G, NS, GW), BF)
    return pl.pallas_call(
        functools.partial(_states_kernel, NC=NC),
        grid=(Bt, nb),
        in_specs=[s_f, s_b, d_f, d_b],
        out_specs=[h_f, h_b],
        out_shape=[hshape, hshape],
        scratch_shapes=[pltpu.VMEM((NS, DI), F32), pltpu.VMEM((NS, DI), F32)],
        compiler_params=pltpu.CompilerParams(
            dimension_semantics=("arbitrary", "arbitrary"), vmem_limit_bytes=VMEM_LIMIT),
        name="ssd_states",
    )(sf, sb, df, db)


def _expansion_matrices():
    sel6 = np.zeros((96, 6 * GW), np.float32)
    sel4 = np.zeros((48, 4 * GW), np.float32)
    for piece in range(3):
        for j in range(8):
            sel6[piece * 32 + j, Q * j:Q * j + Q] = 1.0
        for q in range(4):
            for r in range(HR):
                c0 = GW * q + HP * r
                sel4[piece * 16 + 4 * q + r, c0:c0 + HP] = 1.0
                if q < 2:
                    sel6[piece * 32 + 16 + 4 * q + r, 8 * Q + c0:8 * Q + c0 + HP] = 1.0
    return sel6, sel4


def _scan_kernel(xs_ref, b_ref, c_ref, z_ref, pt_ref, hf_ref, hb_ref, sel_ref, dsk_ref, gnw_ref,
                 y_ref, ex_ref, *, TB):
    NC = TB // Q
    tt = lax.broadcasted_iota(jnp.int32, (Q, Q), 0)
    ss = lax.broadcasted_iota(jnp.int32, (Q, Q), 1)
    lower = ss <= tt
    head_of_lane = lax.broadcasted_iota(jnp.int32, (Q, GW), 1) >> 6
    E1 = 8 * Q
    E2 = 8 * Q + GW

    def gbody(g, carry):
        ex_ref[...] = _tn_dot(_split3(pt_ref[0, g]), sel_ref[...])
        for c in range(NC):
            sl = slice(c * Q, (c + 1) * Q)
            xb = xs_ref[0, g, sl, :]
            cm = c_ref[0, g, sl, :]
            cbm = lax.dot_general(cm, b_ref[0, g, sl, :], (((1,), (1,)), ((), ())),
                                  preferred_element_type=F32)
            ms = []
            xblk = []
            for r in range(HR):
                pt_f = pt_ref[0, g, r:r + 1, sl]
                pt_b = pt_ref[0, g, HR + r:HR + r + 1, sl]
                dt_f = pt_ref[0, g, 8 + r:9 + r, sl]
                dt_b = pt_ref[0, g, 12 + r:13 + r, sl]
                zf = ex_ref[sl, r * Q:(r + 1) * Q] - pt_f
                zb = pt_b - ex_ref[sl, (HR + r) * Q:(HR + r + 1) * Q]
                w = jnp.exp(jnp.where(lower, zf, zb)) * jnp.where(lower, dt_f, dt_b)
                ms.append((cbm * w).astype(BF))
                xblk.append(jnp.where(head_of_lane == r, xb, jnp.zeros_like(xb)))
            y = jnp.dot(jnp.concatenate(ms, axis=1), jnp.concatenate(xblk, axis=0),
                        preferred_element_type=F32)
            hin = jnp.concatenate([hf_ref[0, c, g], hb_ref[0, c, g]], axis=1)
            yoff = jnp.dot(cm, hin, preferred_element_type=F32)
            y = (y + ex_ref[sl, E1:E1 + GW] * yoff[:, :GW] + ex_ref[sl, E2:E2 + GW] * yoff[:, GW:]
                 + dsk_ref[g] * xb.astype(F32))
            y = y * _silu(z_ref[0, g, sl, :].astype(F32))
            y = y * lax.rsqrt(jnp.mean(y * y, axis=-1, keepdims=True) + EPS) * gnw_ref[g]
            y_ref[0, g, sl, :] = y.astype(BF)
        return carry

    lax.fori_loop(0, G, gbody, 0)


def _ssd_scan(xs, bm, cm, z, pt, hf, hb, sel, dsk, gnw, *, TB):
    Bt, _, L, _ = xs.shape
    NC = TB // Q
    gspec = lambda w: pl.BlockSpec((1, G, TB, w), lambda b, i: (b, 0, i, 0))
    hspec = pl.BlockSpec((1, NC, G, NS, GW), lambda b, i: (b, i, 0, 0, 0))
    return pl.pallas_call(
        functools.partial(_scan_kernel, TB=TB),
        grid=(Bt, L // TB),
        in_specs=[gspec(GW), gspec(NS), gspec(NS), gspec(GW),
                  pl.BlockSpec((1, G, 32, TB), lambda b, i: (b, 0, 0, i)), hspec, hspec,
                  _const_spec((96, 6 * GW)), _const_spec((G, 1, GW)), _const_spec((G, 1, GW))],
        out_specs=gspec(GW),
        out_shape=jax.ShapeDtypeStruct((Bt, G, L, GW), BF),
        scratch_shapes=[pltpu.VMEM((TB, 6 * GW), F32)],
        compiler_params=pltpu.CompilerParams(
            dimension_semantics=("arbitrary", "arbitrary"), vmem_limit_bytes=VMEM_LIMIT),
        name="ssd_scan",
    )(xs, bm, cm, z, pt, hf, hb, sel, dsk, gnw)


def _conv_ffn(x1w, keep_prev, keep_next, nw_pre, up_ref, fcw_ref, fcb_ref, down_ref, nw_post,
              flhs_ref, fu_ref, fh_ref, *, TL):
    TW = TL + 16
    rowid = lax.broadcasted_iota(jnp.int32, (TW, 1), 0)
    keep = jnp.where(rowid < 8, keep_prev, jnp.where(rowid >= TL + 8, keep_next, 1.0))
    flhs_ref[...] = (_rms(x1w, nw_pre) * keep).astype(BF)
    for f in range(NF):
        us = fu_ref.at[f % 2]
        us[...] = jnp.dot(flhs_ref[...], up_ref[f], preferred_element_type=F32)
        cw = fcw_ref[f]
        u = us[...]
        v = (pltpu.roll(u, 1, axis=0)[8:8 + TL, :] * cw[0:1, :] + u[8:8 + TL, :] * cw[1:2, :]
             + pltpu.roll(u, TW - 1, axis=0)[8:8 + TL, :] * cw[2:3, :] + fcb_ref[f])
        fh_ref[:, f * FT:(f + 1) * FT] = (_silu(v[:, :FT]) * v[:, FT:]).astype(BF)
    o = jnp.dot(fh_ref[...], down_ref[...], preferred_element_type=F32)
    return x1w[8:8 + TL, :] + _rms(o, nw_post)


_FFN_SCRATCH = lambda TL: [pltpu.VMEM((TL + 16, D), BF), pltpu.VMEM((2, TL + 16, 2 * FT), F32),
                           pltpu.VMEM((TL, FF), BF)]
_FFN_SPECS = lambda: [_const_spec((1, D)), _const_spec((NF, D, 2 * FT)), _const_spec((NF, 3, 2 * FT)),
                      _const_spec((NF, 1, 2 * FT)), _const_spec((FF, D)), _const_spec((1, D))]


def _ssd_out_kernel(x_ref, xp_ref, xq_ref, y_ref, yp_ref, yq_ref, ow_ref, pw_ref,
                    npre_ref, up_ref, fcw_ref, fcb_ref, down_ref, npost_ref,
                    o_ref, ylhs_ref, flhs_ref, fu_ref, fh_ref, *, TL):
    keep_prev, keep_next = _edge_flags()
    for g in range(G):
        ylhs_ref[0:HALO, g * GW:(g + 1) * GW] = yp_ref[0, g]
        ylhs_ref[HALO:HALO + TL, g * GW:(g + 1) * GW] = y_ref[0, g]
        ylhs_ref[HALO + TL:2 * HALO + TL, g * GW:(g + 1) * GW] = yq_ref[0, g]
    h = jnp.dot(ylhs_ref[...], ow_ref[...], preferred_element_type=F32)[8:TL + 24, :]
    xw = jnp.concatenate([xp_ref[0, 8:16, :], x_ref[0], xq_ref[0, 0:8, :]], axis=0)
    x1w = xw + _rms(h, pw_ref[...])
    o_ref[0] = _conv_ffn(x1w, keep_prev, keep_next, npre_ref[...], up_ref, fcw_ref, fcb_ref, down_ref,
                         npost_ref[...], flhs_ref, fu_ref, fh_ref, TL=TL)


def _ssd_out(x, y, ow, pw, npre, up, fcw, fcb, down, npost, *, TL):
    Bt, L, _ = x.shape
    main, prev, nxt = _tile_specs(TL, L, D)
    r = TL // HALO
    last = L // HALO - 1
    ymain = pl.BlockSpec((1, G, TL, GW), lambda b, i: (b, 0, i, 0))
    yprev = pl.BlockSpec((1, G, HALO, GW), lambda b, i: (b, 0, jnp.maximum(i * r - 1, 0), 0))
    ynext = pl.BlockSpec((1, G, HALO, GW), lambda b, i: (b, 0, jnp.minimum((i + 1) * r, last), 0))
    return pl.pallas_call(
        functools.partial(_ssd_out_kernel, TL=TL),
        grid=(Bt, L // TL),
        in_specs=[main, prev, nxt, ymain, yprev, ynext, _const_spec((DI, D)), _const_spec((1, D))]
                 + _FFN_SPECS(),
        out_specs=main,
        out_shape=jax.ShapeDtypeStruct((Bt, L, D), F32),
        scratch_shapes=[pltpu.VMEM((TL + 2 * HALO, DI), BF)] + _FFN_SCRATCH(TL),
        compiler_params=pltpu.CompilerParams(
            dimension_semantics=("arbitrary", "arbitrary"), vmem_limit_bytes=VMEM_LIMIT),
        name="ssd_out",
    )(x, x, x, y, y, y, ow, pw, npre, up, fcw, fcb, down, npost)


CONV_ROWS = 88


def _conformer_kernel(x_ref, xp_ref, xq_ref, nmix_ref, w1_ref, b1_ref, dw_ref, db_ref, lnw_ref, lnb_ref,
                      w2_ref, b2_ref, pw_ref, npre_ref, up_ref, fcw_ref, fcb_ref, down_ref, npost_ref,
                      o_ref, lhs_ref, g_ref, cv_ref, l2_ref, flhs_ref, fu_ref, fh_ref, *, TL):
    keep_prev, keep_next = _edge_flags()
    TA = TL + 2 * HALO
    TW = TL + 16
    nmix = nmix_ref[...]
    lhs_ref[0:HALO, :] = _rms(xp_ref[0], nmix).astype(BF)
    lhs_ref[HALO:HALO + TL, :] = _rms(x_ref[0], nmix).astype(BF)
    lhs_ref[HALO + TL:TA, :] = _rms(xq_ref[0], nmix).astype(BF)
    rowid = lax.broadcasted_iota(jnp.int32, (TA, 1), 0)
    keep = jnp.where(rowid < HALO, keep_prev, jnp.where(rowid >= HALO + TL, keep_next, 1.0))
    half = D // 2
    for hb in range(2):
        cs = slice(hb * half, (hb + 1) * half)
        a = jnp.dot(lhs_ref[...], w1_ref[:, cs], preferred_element_type=F32) + b1_ref[:, cs]
        gt = jnp.dot(lhs_ref[...], w1_ref[:, D + hb * half:D + (hb + 1) * half],
                     preferred_element_type=F32) + b1_ref[:, D + hb * half:D + (hb + 1) * half]
        g_ref[8:8 + TA, cs] = a * _sigmoid(gt) * keep
    g_ref[0:8, :] = jnp.zeros((8, D), F32)
    g_ref[8 + TA:16 + TA, :] = jnp.zeros((8, D), F32)
    nz = CONV_ROWS + 8
    for r0 in range(0, TW, CONV_ROWS):
        for lb in range(D // 128):
            ls = slice(lb * 128, (lb + 1) * 128)
            gblk = g_ref[r0:r0 + CONV_ROWS + 32, ls]
            acc = None
            for b in range(8):
                zb = None
                for a in range(4):
                    k = 8 * a + b - 1
                    if k < 0:
                        continue
                    term = gblk[8 * a:8 * a + nz, :] * dw_ref[k:k + 1, ls]
                    zb = term if zb is None else zb + term
                if b:
                    zb = pltpu.roll(zb, nz - b, axis=0)
                zb = zb[0:CONV_ROWS, :]
                acc = zb if acc is None else acc + zb
            cv_ref[r0:r0 + CONV_ROWS, ls] = acc
    cv = cv_ref[...] + db_ref[...]
    mu = jnp.mean(cv, axis=-1, keepdims=True)
    xc = cv - mu
    ln = xc * lax.rsqrt(jnp.mean(xc * xc, axis=-1, keepdims=True) + EPS) * lnw_ref[...] + lnb_ref[...]
    l2_ref[...] = _silu(ln).astype(BF)
    h = jnp.dot(l2_ref[...], w2_ref[...], preferred_element_type=F32) + b2_ref[...]
    xw = jnp.concatenate([xp_ref[0, 8:16, :], x_ref[0], xq_ref[0, 0:8, :]], axis=0)
    x1w = xw + _rms(h, pw_ref[...])
    o_ref[0] = _conv_ffn(x1w, keep_prev, keep_next, npre_ref[...], up_ref, fcw_ref, fcb_ref, down_ref,
                         npost_ref[...], flhs_ref, fu_ref, fh_ref, TL=TL)


def _conformer(x, nmix, w1, b1, dw, db, lnw, lnb, w2, b2, pw, npre, up, fcw, fcb, down, npost, *, TL):
    Bt, L, _ = x.shape
    assert (TL + 16) % CONV_ROWS == 0
    main, prev, nxt = _tile_specs(TL, L, D)
    return pl.pallas_call(
        functools.partial(_conformer_kernel, TL=TL),
        grid=(Bt, L // TL),
        in_specs=[main, prev, nxt, _const_spec((1, D)), _const_spec((D, 2 * D)), _const_spec((1, 2 * D)),
                  _const_spec((CFK, D)), _const_spec((1, D)), _const_spec((1, D)), _const_spec((1, D)),
                  _const_spec((D, D)), _const_spec((1, D)), _const_spec((1, D))] + _FFN_SPECS(),
        out_specs=main,
        out_shape=jax.ShapeDtypeStruct((Bt, L, D), F32),
        scratch_shapes=[pltpu.VMEM((TL + 2 * HALO, D), BF), pltpu.VMEM((TL + 2 * HALO + 16, D), F32),
                        pltpu.VMEM((TL + 16, D), F32), pltpu.VMEM((TL + 16, D), BF)] + _FFN_SCRATCH(TL),
        compiler_params=pltpu.CompilerParams(
            dimension_semantics=("arbitrary", "arbitrary"), vmem_limit_bytes=VMEM_LIMIT),
        name="conformer",
    )(x, x, x, nmix, w1, b1, dw, db, lnw, lnb, w2, b2, pw, npre, up, fcw, fcb, down, npost)


def _ffn_params(i, ffn_up_w, ffn_conv_w, ffn_conv_b, ffn_down_w, norm_pre_ffn, norm_post_ffn):
    def pair(a):
        lead = a.shape[:-1]
        a = a.reshape(lead + (2, NF, FT))
        a = jnp.moveaxis(a, -2, 0)
        return a.reshape((NF,) + lead + (2 * FT,))
    return (norm_pre_ffn[i][None, :], pair(ffn_up_w[i]).astype(BF), pair(ffn_conv_w[i]),
            pair(ffn_conv_b[i][None, :]), ffn_down_w[i].astype(BF), norm_post_ffn[i][None, :])


def kernel(x_prompt, x_sample, ssm_in_w, ssm_conv_w, ssm_conv_b, ssm_dt_bias, ssm_a_log, ssm_d, ssm_norm_w, ssm_out_w, cf_pw1_w, cf_pw1_b, cf_dw_w, cf_dw_b, cf_ln_w, cf_ln_b, cf_pw2_w, cf_pw2_b, ffn_up_w, ffn_conv_w, ffn_conv_b, ffn_down_w, norm_pre_mix, norm_post_mix, norm_pre_ffn, norm_post_ffn):
    TL = 512
    TB = 512
    in_w = ssm_in_w[0]
    wz = in_w[:, :DI].reshape(D, G, GW).transpose(1, 0, 2).astype(BF)
    wx = in_w[:, DI:DI + 2 * DI].reshape(D, 2 * G, GW).transpose(1, 0, 2).astype(BF)
    perm = np.array([d * NH + HR * g + r for g in range(G) for d in range(2) for r in range(HR)])
    wdt = in_w[:, DI + 2 * DI:][:, perm].T.astype(BF)
    dtb = jnp.broadcast_to(ssm_dt_bias[0].reshape(-1)[perm][:, None], (2 * NH, TL))
    alog = jnp.broadcast_to(ssm_a_log[0].reshape(-1)[perm][:, None], (2 * NH, TL))
    cw = ssm_conv_w[0].reshape(3, 2 * G, GW).transpose(1, 0, 2)
    cb = ssm_conv_b[0].reshape(2 * G, 1, GW)
    sel6_np, sel4_np = _expansion_matrices()
    sel6 = jnp.asarray(sel6_np, BF)
    sel4 = jnp.asarray(sel4_np, BF)
    dsk = jnp.repeat(ssm_d[0], HP).reshape(G, 1, GW)
    gnw = ssm_norm_w[0].reshape(G, 1, GW)
    ow = ssm_out_w[0].astype(BF)
    ffn0 = _ffn_params(0, ffn_up_w, ffn_conv_w, ffn_conv_b, ffn_down_w, norm_pre_ffn, norm_post_ffn)
    ffn1 = _ffn_params(1, ffn_up_w, ffn_conv_w, ffn_conv_b, ffn_down_w, norm_pre_ffn, norm_post_ffn)
    w1 = cf_pw1_w[0].astype(BF)
    w2 = cf_pw2_w[0].astype(BF)
    row = lambda a: a[None, :]

    def layer_stack(x):
        z, xs, bm, cm, pt, sf, sb, df, db = _ssd_in(
            x, row(norm_pre_mix[0]), wz, wx, wdt, cw, cb, dtb, alog, sel4, TL=TL)
        hf, hb = _ssd_states(sf, sb, df, db, NC=TL // Q)
        y = _ssd_scan(xs, bm, cm, z, pt, hf, hb, sel6, dsk, gnw, TB=TB)
        x = _ssd_out(x, y, ow, row(norm_post_mix[0]), *ffn0, TL=TL)
        x = _conformer(x, row(norm_pre_mix[1]), w1, row(cf_pw1_b[0]), cf_dw_w[0], row(cf_dw_b[0]),
                       row(cf_ln_w[0]), row(cf_ln_b[0]), w2, row(cf_pw2_b[0]), row(norm_post_mix[1]),
                       *ffn1, TL=TL)
        return x

    return (layer_stack(x_prompt), layer_stack(x_sample))
```

```python
import functools

import numpy as np
import jax
import jax.numpy as jnp
from jax import lax
from jax.experimental import pallas as pl
from jax.experimental.pallas import tpu as pltpu

F32 = jnp.float32
BF = jnp.bfloat16

D = 1024
DI = 2048
G = 8
NS = 128
NH = 32
HP = 64
HR = NH // G
GW = HR * HP
Q = 128
FF = 2816
FT = 256
NF = FF // FT
CFK = 31
EPS = 1e-6
HALO = 16
VMEM_LIMIT = 56 * 1024 * 1024


def _rms(x, w):
    return x * lax.rsqrt(jnp.mean(x * x, axis=-1, keepdims=True) + EPS) * w


def _sigmoid(x):
    return 1.0 / (1.0 + jnp.exp(-x))


def _silu(x):
    return x * _sigmoid(x)


def _softplus(x):
    return jnp.maximum(x, 0.0) + jnp.log1p(jnp.exp(-jnp.abs(x)))


def _split3(v):
    hi = v.astype(BF)
    r1 = v - hi.astype(F32)
    mid = r1.astype(BF)
    lo = (r1 - mid.astype(F32)).astype(BF)
    return jnp.concatenate([hi, mid, lo], axis=0)


def _tn_dot(a, b):
    return lax.dot_general(a, b, (((0,), (0,)), ((), ())), preferred_element_type=F32)


def _const_spec(shape):
    nd = len(shape)
    return pl.BlockSpec(shape, lambda *_: (0,) * nd, pipeline_mode=pl.Buffered(1))


def _tile_specs(TL, L, width):
    r = TL // HALO
    last = L // HALO - 1
    main = pl.BlockSpec((1, TL, width), lambda b, i: (b, i, 0))
    prev = pl.BlockSpec((1, HALO, width), lambda b, i: (b, jnp.maximum(i * r - 1, 0), 0))
    nxt = pl.BlockSpec((1, HALO, width), lambda b, i: (b, jnp.minimum((i + 1) * r, last), 0))
    return main, prev, nxt


def _edge_flags():
    i = pl.program_id(1)
    nt = pl.num_programs(1)
    keep_prev = jnp.where(i > 0, jnp.float32(1), jnp.float32(0))
    keep_next = jnp.where(i < nt - 1, jnp.float32(1), jnp.float32(0))
    return keep_prev, keep_next


def _expansion_matrices():
    sel6 = np.zeros((96, 6 * GW), np.float32)
    sel4 = np.zeros((48, 4 * GW), np.float32)
    for piece in range(3):
        for j in range(8):
            sel6[piece * 32 + j, Q * j:Q * j + Q] = 1.0
        for q in range(4):
            for r in range(HR):
                c0 = GW * q + HP * r
                sel4[piece * 16 + 4 * q + r, c0:c0 + HP] = 1.0
                if q < 2:
                    sel6[piece * 32 + 16 + 4 * q + r, 8 * Q + c0:8 * Q + c0 + HP] = 1.0
    return sel6, sel4


def _ssd_in_kernel(x_ref, xp_ref, xq_ref, nw_ref, wz_ref, wx_ref, wdt_ref, cw_ref, cb_ref,
                   dtb_ref, alog_ref, sel4_ref,
                   z_ref, xs_ref, b_ref, c_ref, pt_ref, hf_out, sb_ref, db_ref,
                   lhs_ref, u_ref, e4_ref, hf_ref, *, TL):
    keep_prev, keep_next = _edge_flags()
    NC = TL // Q

    @pl.when(pl.program_id(1) == 0)
    def _():
        hf_ref[...] = jnp.zeros(hf_ref.shape, F32)

    nw = nw_ref[...]
    lhs_ref[0:HALO, :] = (_rms(xp_ref[0], nw) * keep_prev).astype(BF)
    lhs_ref[HALO:HALO + TL, :] = _rms(x_ref[0], nw).astype(BF)
    lhs_ref[HALO + TL:2 * HALO + TL, :] = (_rms(xq_ref[0], nw) * keep_next).astype(BF)
    xm = lhs_ref[HALO:HALO + TL, :]

    dtr = lax.dot_general(wdt_ref[...], xm, (((1,), (1,)), ((), ())), preferred_element_type=F32)
    for g in range(G):
        z_ref[0, g] = jnp.dot(xm, wz_ref[:, g * GW:(g + 1) * GW], preferred_element_type=F32).astype(BF)
    dt = _softplus(dtr + dtb_ref[...])
    da = dt * (-jnp.exp(alog_ref[...]))
    lane = lax.broadcasted_iota(jnp.int32, (2 * NH, TL), 1) & (Q - 1)
    cs = da
    sh = 1
    while sh < Q:
        cs = cs + jnp.where(lane >= sh, pltpu.roll(cs, sh, axis=1), 0.0)
        sh *= 2
    tot = jnp.concatenate(
        [jnp.broadcast_to(cs[:, c * Q + Q - 1:c * Q + Q], (2 * NH, Q)) for c in range(NC)], axis=1)
    row = lax.broadcasted_iota(jnp.int32, (2 * NH, TL), 0)
    isf = (row & 7) < HR
    ex = cs - da
    q0 = jnp.where(isf, cs, ex)
    q2 = jnp.exp(jnp.where(isf, cs, tot - ex))
    q3 = dt * jnp.exp(jnp.where(isf, tot - cs, ex))
    for g in range(G):
        pt_ref[0, g, 0:8, :] = q0[8 * g:8 * g + 8, :]
        pt_ref[0, g, 8:16, :] = dt[8 * g:8 * g + 8, :]
        pt_ref[0, g, 16:24, :] = q2[8 * g:8 * g + 8, :]
        pt_ref[0, g, 24:32, :] = q3[8 * g:8 * g + 8, :]
    db_ref[...] = jnp.zeros(db_ref.shape, F32)

    def conv_block(j, buf):
        js = slice(j * GW, (j + 1) * GW)
        us = u_ref.at[buf]
        us[...] = jnp.dot(lhs_ref[...], wx_ref[:, js], preferred_element_type=F32)
        u = us[...]
        v = (pltpu.roll(u, 1, axis=0)[HALO:HALO + TL, :] * cw_ref[0:1, js]
             + u[HALO:HALO + TL, :] * cw_ref[1:2, js]
             + pltpu.roll(u, TL + 2 * HALO - 1, axis=0)[HALO:HALO + TL, :] * cw_ref[2:3, js]
             + cb_ref[:, js])
        return _silu(v)

    for jj in range(G // 2):
        v = conv_block(G + jj, 0).astype(BF)
        b_ref[0, 2 * jj] = v[:, :NS]
        b_ref[0, 2 * jj + 1] = v[:, NS:]
        v = conv_block(G + G // 2 + jj, 1).astype(BF)
        c_ref[0, 2 * jj] = v[:, :NS]
        c_ref[0, 2 * jj + 1] = v[:, NS:]

    for g in range(G):
        v = conv_block(g, g % 2)
        xs_ref[0, g] = v.astype(BF)
        tab = jnp.concatenate([q2[8 * g:8 * g + 8, :], q3[8 * g:8 * g + 8, :]], axis=0)
        es = e4_ref.at[g % 2]
        es[...] = _tn_dot(_split3(tab), sel4_ref[...])
        h = hf_ref[g]
        for c in range(NC):
            sl = slice(c * Q, (c + 1) * Q)
            e4 = es[sl, :]
            xv = v[sl, :]
            xw = jnp.concatenate([(xv * e4[:, 2 * GW:3 * GW]).astype(BF),
                                  (xv * e4[:, 3 * GW:4 * GW]).astype(BF)], axis=1)
            s = _tn_dot(b_ref[0, g, sl, :], xw)
            hf_out[0, c, g] = h.astype(BF)
            h = e4[Q - 1:Q, 0:GW] * h + s[:, :GW]
            sb_ref[0, c, g] = s[:, GW:].astype(BF)
            db_ref[0, 0, g, c:c + 1, :] = e4[0:1, GW:2 * GW]
        hf_ref[g] = h


def _ssd_in(x, nw, wz, wx, wdt, cw, cb, dtb, alog, sel4, *, TL):
    Bt, L, _ = x.shape
    NC = TL // Q
    main, prev, nxt = _tile_specs(TL, L, D)
    gspec = lambda w: pl.BlockSpec((1, G, TL, w), lambda b, i: (b, 0, i, 0))
    hspec = pl.BlockSpec((1, NC, G, NS, GW), lambda b, i: (b, i, 0, 0, 0))
    dspec = pl.BlockSpec((1, 1, G, 8, GW), lambda b, i: (b, i, 0, 0, 0))
    hshape = jax.ShapeDtypeStruct((Bt, L // Q, G, NS, GW), BF)
    return pl.pallas_call(
        functools.partial(_ssd_in_kernel, TL=TL),
        grid=(Bt, L // TL),
        in_specs=[main, prev, nxt, _const_spec((1, D)), _const_spec((D, DI)),
                  _const_spec((D, 2 * DI)), _const_spec((2 * NH, D)),
                  _const_spec((3, 2 * DI)), _const_spec((1, 2 * DI)),
                  _const_spec((2 * NH, TL)), _const_spec((2 * NH, TL)), _const_spec((48, 4 * GW))],
        out_specs=[gspec(GW), gspec(GW), gspec(NS), gspec(NS),
                   pl.BlockSpec((1, G, 32, TL), lambda b, i: (b, 0, 0, i)),
                   hspec, hspec, dspec],
        out_shape=[jax.ShapeDtypeStruct((Bt, G, L, GW), BF), jax.ShapeDtypeStruct((Bt, G, L, GW), BF),
                   jax.ShapeDtypeStruct((Bt, G, L, NS), BF), jax.ShapeDtypeStruct((Bt, G, L, NS), BF),
                   jax.ShapeDtypeStruct((Bt, G, 32, L), F32),
                   hshape, hshape,
                   jax.ShapeDtypeStruct((Bt, L // TL, G, 8, GW), F32)],
        scratch_shapes=[pltpu.VMEM((TL + 2 * HALO, D), BF), pltpu.VMEM((2, TL + 2 * HALO, GW), F32),
                        pltpu.VMEM((2, TL, 4 * GW), F32), pltpu.VMEM((G, NS, GW), F32)],
        compiler_params=pltpu.CompilerParams(
            dimension_semantics=("arbitrary", "arbitrary"), vmem_limit_bytes=VMEM_LIMIT),
        name="ssd_in",
    )(x, x, x, nw, wz, wx, wdt, cw, cb, dtb, alog, sel4)


def _scan_kernel(xs_ref, b_ref, c_ref, z_ref, pt_ref, hf_ref, sb_ref, db_ref, sel_ref, dsk_ref, gnw_ref,
                 y_ref, exs_ref, hinb_ref, hb_ref, *, TB):
    NC = TB // Q

    @pl.when(pl.program_id(1) == 0)
    def _():
        hb_ref[...] = jnp.zeros(hb_ref.shape, F32)

    tt = lax.broadcasted_iota(jnp.int32, (Q, Q), 0)
    ss = lax.broadcasted_iota(jnp.int32, (Q, Q), 1)
    lower = ss <= tt
    head_of_lane = lax.broadcasted_iota(jnp.int32, (Q, GW), 1) >> 6
    E1 = 8 * Q
    E2 = 8 * Q + GW

    def do_group(g, ex_ref, hin_ref):
        ex_ref[...] = _tn_dot(_split3(pt_ref[0, g]), sel_ref[...])
        h = hb_ref[g]
        for c in reversed(range(NC)):
            hin_ref[c] = h.astype(BF)
            h = db_ref[0, 0, g, c:c + 1, :] * h + sb_ref[0, c, g].astype(F32)
        hb_ref[g] = h
        for c in range(NC):
            sl = slice(c * Q, (c + 1) * Q)
            xb = xs_ref[0, g, sl, :]
            cm = c_ref[0, g, sl, :]
            cbm = lax.dot_general(cm, b_ref[0, g, sl, :], (((1,), (1,)), ((), ())),
                                  preferred_element_type=F32)
            ms = []
            xblk = []
            for r in range(HR):
                pt_f = pt_ref[0, g, r:r + 1, sl]
                pt_b = pt_ref[0, g, HR + r:HR + r + 1, sl]
                dt_f = pt_ref[0, g, 8 + r:9 + r, sl]
                dt_b = pt_ref[0, g, 12 + r:13 + r, sl]
                zf = ex_ref[sl, r * Q:(r + 1) * Q] - pt_f
                zb = pt_b - ex_ref[sl, (HR + r) * Q:(HR + r + 1) * Q]
                w = jnp.exp(jnp.where(lower, zf, zb)) * jnp.where(lower, dt_f, dt_b)
                ms.append((cbm * w).astype(BF))
                xblk.append(jnp.where(head_of_lane == r, xb, jnp.zeros_like(xb)))
            y = jnp.dot(jnp.concatenate(ms, axis=1), jnp.concatenate(xblk, axis=0),
                        preferred_element_type=F32)
            hin = jnp.concatenate([hf_ref[0, c, g], hin_ref[c]], axis=1)
            yoff = jnp.dot(cm, hin, preferred_element_type=F32)
            y = (y + ex_ref[sl, E1:E1 + GW] * yoff[:, :GW] + ex_ref[sl, E2:E2 + GW] * yoff[:, GW:]
                 + dsk_ref[g] * xb.astype(F32))
            y = y * _silu(z_ref[0, g, sl, :].astype(F32))
            y = y * lax.rsqrt(jnp.mean(y * y, axis=-1, keepdims=True) + EPS) * gnw_ref[g]
            y_ref[0, g, sl, :] = y.astype(BF)

    def pair_body(k, carry):
        do_group(2 * k, exs_ref.at[0], hinb_ref.at[0])
        do_group(2 * k + 1, exs_ref.at[1], hinb_ref.at[1])
        return carry

    lax.fori_loop(0, G // 2, pair_body, 0)


def _ssd_scan(xs, bm, cm, z, pt, hf, sb, db, sel, dsk, gnw, *, TB):
    Bt, _, L, _ = xs.shape
    NC = TB // Q
    nb = L // TB
    gspec = lambda w: pl.BlockSpec((1, G, TB, w), lambda b, i: (b, 0, nb - 1 - i, 0))
    hspec = pl.BlockSpec((1, NC, G, NS, GW), lambda b, i: (b, nb - 1 - i, 0, 0, 0))
    return pl.pallas_call(
        functools.partial(_scan_kernel, TB=TB),
        grid=(Bt, nb),
        in_specs=[gspec(GW), gspec(NS), gspec(NS), gspec(GW),
                  pl.BlockSpec((1, G, 32, TB), lambda b, i: (b, 0, 0, nb - 1 - i)), hspec, hspec,
                  pl.BlockSpec((1, 1, G, 8, GW), lambda b, i: (b, nb - 1 - i, 0, 0, 0)),
                  _const_spec((96, 6 * GW)), _const_spec((G, 1, GW)), _const_spec((G, 1, GW))],
        out_specs=gspec(GW),
        out_shape=jax.ShapeDtypeStruct((Bt, G, L, GW), BF),
        scratch_shapes=[pltpu.VMEM((2, TB, 6 * GW), F32), pltpu.VMEM((2, NC, NS, GW), BF),
                        pltpu.VMEM((G, NS, GW), F32)],
        compiler_params=pltpu.CompilerParams(
            dimension_semantics=("arbitrary", "arbitrary"), vmem_limit_bytes=VMEM_LIMIT),
        name="ssd_scan",
    )(xs, bm, cm, z, pt, hf, sb, db, sel, dsk, gnw)


def _conv_ffn(x1w, keep_prev, keep_next, nw_pre, up_ref, fcw_ref, fcb_ref, down_ref, nw_post,
              flhs_ref, fu_ref, fh_ref, *, TL):
    TW = TL + 16
    rowid = lax.broadcasted_iota(jnp.int32, (TW, 1), 0)
    keep = jnp.where(rowid < 8, keep_prev, jnp.where(rowid >= TL + 8, keep_next, 1.0))
    flhs_ref[...] = (_rms(x1w, nw_pre) * keep).astype(BF)
    for f in range(NF):
        gate = slice(f * FT, (f + 1) * FT)
        val = slice(FF + f * FT, FF + (f + 1) * FT)
        us = fu_ref.at[f % 2]
        us[:, :FT] = jnp.dot(flhs_ref[...], up_ref[:, gate], preferred_element_type=F32)
        us[:, FT:] = jnp.dot(flhs_ref[...], up_ref[:, val], preferred_element_type=F32)
        cw = jnp.concatenate([fcw_ref[:, gate], fcw_ref[:, val]], axis=1)
        cb = jnp.concatenate([fcb_ref[:, gate], fcb_ref[:, val]], axis=1)
        u = us[...]
        v = (pltpu.roll(u, 1, axis=0)[8:8 + TL, :] * cw[0:1, :] + u[8:8 + TL, :] * cw[1:2, :]
             + pltpu.roll(u, TW - 1, axis=0)[8:8 + TL, :] * cw[2:3, :] + cb)
        fh_ref[:, gate] = (_silu(v[:, :FT]) * v[:, FT:]).astype(BF)
    o = jnp.dot(fh_ref[...], down_ref[...], preferred_element_type=F32)
    return x1w[8:8 + TL, :] + _rms(o, nw_post)


_FFN_SCRATCH = lambda TL: [pltpu.VMEM((TL + 16, D), BF), pltpu.VMEM((2, TL + 16, 2 * FT), F32),
                           pltpu.VMEM((TL, FF), BF)]
_FFN_SPECS = lambda: [_const_spec((1, D)), _const_spec((D, 2 * FF)), _const_spec((3, 2 * FF)),
                      _const_spec((1, 2 * FF)), _const_spec((FF, D)), _const_spec((1, D))]


def _ssd_out_kernel(x_ref, xp_ref, xq_ref, y_ref, yp_ref, yq_ref, ow_ref, pw_ref,
                    npre_ref, up_ref, fcw_ref, fcb_ref, down_ref, npost_ref,
                    o_ref, ylhs_ref, flhs_ref, fu_ref, fh_ref, *, TL):
    keep_prev, keep_next = _edge_flags()
    for g in range(G):
        ylhs_ref[0:HALO, g * GW:(g + 1) * GW] = yp_ref[0, g]
        ylhs_ref[HALO:HALO + TL, g * GW:(g + 1) * GW] = y_ref[0, g]
        ylhs_ref[HALO + TL:2 * HALO + TL, g * GW:(g + 1) * GW] = yq_ref[0, g]
    h = jnp.dot(ylhs_ref[...], ow_ref[...], preferred_element_type=F32)[8:TL + 24, :]
    xw = jnp.concatenate([xp_ref[0, 8:16, :], x_ref[0], xq_ref[0, 0:8, :]], axis=0)
    x1w = xw + _rms(h, pw_ref[...])
    o_ref[0] = _conv_ffn(x1w, keep_prev, keep_next, npre_ref[...], up_ref, fcw_ref, fcb_ref, down_ref,
                         npost_ref[...], flhs_ref, fu_ref, fh_ref, TL=TL)


def _ssd_out(x, y, ow, pw, npre, up, fcw, fcb, down, npost, *, TL):
    Bt, L, _ = x.shape
    main, prev, nxt = _tile_specs(TL, L, D)
    r = TL // HALO
    last = L // HALO - 1
    ymain = pl.BlockSpec((1, G, TL, GW), lambda b, i: (b, 0, i, 0))
    yprev = pl.BlockSpec((1, G, HALO, GW), lambda b, i: (b, 0, jnp.maximum(i * r - 1, 0), 0))
    ynext = pl.BlockSpec((1, G, HALO, GW), lambda b, i: (b, 0, jnp.minimum((i + 1) * r, last), 0))
    return pl.pallas_call(
        functools.partial(_ssd_out_kernel, TL=TL),
        grid=(Bt, L // TL),
        in_specs=[main, prev, nxt, ymain, yprev, ynext, _const_spec((DI, D)), _const_spec((1, D))]
                 + _FFN_SPECS(),
        out_specs=main,
        out_shape=jax.ShapeDtypeStruct((Bt, L, D), F32),
        scratch_shapes=[pltpu.VMEM((TL + 2 * HALO, DI), BF)] + _FFN_SCRATCH(TL),
        compiler_params=pltpu.CompilerParams(
            dimension_semantics=("arbitrary", "arbitrary"), vmem_limit_bytes=VMEM_LIMIT),
        name="ssd_out",
    )(x, x, x, y, y, y, ow, pw, npre, up, fcw, fcb, down, npost)


CONV_ROWS = 88


def _conformer_kernel(x_ref, xp_ref, xq_ref, nmix_ref, w1_ref, b1_ref, dw_ref, db_ref, lnw_ref, lnb_ref,
                      w2_ref, b2_ref, pw_ref, npre_ref, up_ref, fcw_ref, fcb_ref, down_ref, npost_ref,
                      o_ref, lhs_ref, g_ref, cv_ref, l2_ref, flhs_ref, fu_ref, fh_ref, *, TL):
    keep_prev, keep_next = _edge_flags()
    TA = TL + 2 * HALO
    TW = TL + 16
    nmix = nmix_ref[...]
    lhs_ref[0:HALO, :] = _rms(xp_ref[0], nmix).astype(BF)
    lhs_ref[HALO:HALO + TL, :] = _rms(x_ref[0], nmix).astype(BF)
    lhs_ref[HALO + TL:TA, :] = _rms(xq_ref[0], nmix).astype(BF)
    rowid = lax.broadcasted_iota(jnp.int32, (TA, 1), 0)
    keep = jnp.where(rowid < HALO, keep_prev, jnp.where(rowid >= HALO + TL, keep_next, 1.0))
    half = D // 2
    for hb in range(2):
        cs = slice(hb * half, (hb + 1) * half)
        a = jnp.dot(lhs_ref[...], w1_ref[:, cs], preferred_element_type=F32) + b1_ref[:, cs]
        gt = jnp.dot(lhs_ref[...], w1_ref[:, D + hb * half:D + (hb + 1) * half],
                     preferred_element_type=F32) + b1_ref[:, D + hb * half:D + (hb + 1) * half]
        g_ref[8:8 + TA, cs] = a * _sigmoid(gt) * keep
    g_ref[0:8, :] = jnp.zeros((8, D), F32)
    g_ref[8 + TA:16 + TA, :] = jnp.zeros((8, D), F32)
    nz = CONV_ROWS + 8
    for r0 in range(0, TW, CONV_ROWS):
        for lb in range(D // 128):
            ls = slice(lb * 128, (lb + 1) * 128)
            gblk = g_ref[r0:r0 + CONV_ROWS + 32, ls]
            acc = None
            for b in range(8):
                zb = None
                for a in range(4):
                    k = 8 * a + b - 1
                    if k < 0:
                        continue
                    term = gblk[8 * a:8 * a + nz, :] * dw_ref[k:k + 1, ls]
                    zb = term if zb is None else zb + term
                if b:
                    zb = pltpu.roll(zb, nz - b, axis=0)
                zb = zb[0:CONV_ROWS, :]
                acc = zb if acc is None else acc + zb
            cv_ref[r0:r0 + CONV_ROWS, ls] = acc
    cv = cv_ref[...] + db_ref[...]
    mu = jnp.mean(cv, axis=-1, keepdims=True)
    xc = cv - mu
    ln = xc * lax.rsqrt(jnp.mean(xc * xc, axis=-1, keepdims=True) + EPS) * lnw_ref[...] + lnb_ref[...]
    l2_ref[...] = _silu(ln).astype(BF)
    h = jnp.dot(l2_ref[...], w2_ref[...], preferred_element_type=F32) + b2_ref[...]
    xw = jnp.concatenate([xp_ref[0, 8:16, :], x_ref[0], xq_ref[0, 0:8, :]], axis=0)
    x1w = xw + _rms(h, pw_ref[...])
    o_ref[0] = _conv_ffn(x1w, keep_prev, keep_next, npre_ref[...], up_ref, fcw_ref, fcb_ref, down_ref,
                         npost_ref[...], flhs_ref, fu_ref, fh_ref, TL=TL)


def _conformer(x, nmix, w1, b1, dw, db, lnw, lnb, w2, b2, pw, npre, up, fcw, fcb, down, npost, *, TL):
    Bt, L, _ = x.shape
    assert (TL + 16) % CONV_ROWS == 0
    main, prev, nxt = _tile_specs(TL, L, D)
    return pl.pallas_call(
        functools.partial(_conformer_kernel, TL=TL),
        grid=(Bt, L // TL),
        in_specs=[main, prev, nxt, _const_spec((1, D)), _const_spec((D, 2 * D)), _const_spec((1, 2 * D)),
                  _const_spec((CFK, D)), _const_spec((1, D)), _const_spec((1, D)), _const_spec((1, D)),
                  _const_spec((D, D)), _const_spec((1, D)), _const_spec((1, D))] + _FFN_SPECS(),
        out_specs=main,
        out_shape=jax.ShapeDtypeStruct((Bt, L, D), F32),
        scratch_shapes=[pltpu.VMEM((TL + 2 * HALO, D), BF), pltpu.VMEM((TL + 2 * HALO + 16, D), F32),
                        pltpu.VMEM((TL + 16, D), F32), pltpu.VMEM((TL + 16, D), BF)] + _FFN_SCRATCH(TL),
        compiler_params=pltpu.CompilerParams(
            dimension_semantics=("arbitrary", "arbitrary"), vmem_limit_bytes=VMEM_LIMIT),
        name="conformer",
    )(x, x, x, nmix, w1, b1, dw, db, lnw, lnb, w2, b2, pw, npre, up, fcw, fcb, down, npost)


def kernel(x_prompt, x_sample, ssm_in_w, ssm_conv_w, ssm_conv_b, ssm_dt_bias, ssm_a_log, ssm_d, ssm_norm_w, ssm_out_w, cf_pw1_w, cf_pw1_b, cf_dw_w, cf_dw_b, cf_ln_w, cf_ln_b, cf_pw2_w, cf_pw2_b, ffn_up_w, ffn_conv_w, ffn_conv_b, ffn_down_w, norm_pre_mix, norm_post_mix, norm_pre_ffn, norm_post_ffn):
    TL = 512
    TB = 512
    row = lambda a: a[None, :]
    in_w = ssm_in_w[0]
    wz = in_w[:, :DI].astype(BF)
    wx = in_w[:, DI:3 * DI].astype(BF)
    perm = np.array([d * NH + HR * g + r for g in range(G) for d in range(2) for r in range(HR)])
    wdt = in_w[:, 3 * DI:][:, perm].T.astype(BF)
    dtb = jnp.broadcast_to(ssm_dt_bias[0].reshape(-1)[perm][:, None], (2 * NH, TL))
    alog = jnp.broadcast_to(ssm_a_log[0].reshape(-1)[perm][:, None], (2 * NH, TL))
    sel6_np, sel4_np = _expansion_matrices()
    sel6 = jnp.asarray(sel6_np, BF)
    sel4 = jnp.asarray(sel4_np, BF)
    dsk = jnp.repeat(ssm_d[0], HP).reshape(G, 1, GW)
    gnw = ssm_norm_w[0].reshape(G, 1, GW)
    ow = ssm_out_w[0].astype(BF)
    ffn = [(row(norm_pre_ffn[i]), ffn_up_w[i].astype(BF), ffn_conv_w[i], row(ffn_conv_b[i]),
            ffn_down_w[i].astype(BF), row(norm_post_ffn[i])) for i in range(2)]
    w1 = cf_pw1_w[0].astype(BF)
    w2 = cf_pw2_w[0].astype(BF)

    def layer_stack(x):
        z, xs, bm, cm, pt, hf, sb, db = _ssd_in(
            x, row(norm_pre_mix[0]), wz, wx, wdt, ssm_conv_w[0], row(ssm_conv_b[0]), dtb, alog, sel4, TL=TL)
        y = _ssd_scan(xs, bm, cm, z, pt, hf, sb, db, sel6, dsk, gnw, TB=TB)
        x = _ssd_out(x, y, ow, row(norm_post_mix[0]), *ffn[0], TL=TL)
        x = _conformer(x, row(norm_pre_mix[1]), w1, row(cf_pw1_b[0]), cf_dw_w[0], row(cf_dw_b[0]),
                       row(cf_ln_w[0]), row(cf_ln_b[0]), w2, row(cf_pw2_b[0]), row(norm_post_mix[1]),
                       *ffn[1], TL=TL)
        return x

    return (layer_stack(x_prompt), layer_stack(x_sample))
```
